```python
import math
import jax, jax.numpy as jnp
from jax import lax
import numpy as np

D_MODEL = 1024
BATCH = 8
SEQ = 2048
DEPTH = 4

N_HEADS = 8
QK_NOPE = 128
QK_ROPE = 64
V_HEAD = 128
Q_LORA = 384
KV_LORA = 256
ATTN_WIDTH = N_HEADS * V_HEAD
ROPE_THETA = 10000.0
Q_BLOCK = 128
CONV_WIDTH = D_MODEL
CONV_K = 31
RMS_EPS = 1e-6
LN_EPS = 1e-5
N_MIXERS = 2
N_MLA = (DEPTH + 1) // 2
N_CONV = DEPTH // 2
MLA_IN = Q_LORA + KV_LORA + QK_ROPE + ATTN_WIDTH

kernel_name = "hybrid_mla_conformer_conv_gated"


def rmsnorm(x, g):
    xf = x.astype(jnp.float32)
    y = xf * lax.rsqrt(jnp.mean(xf * xf, axis=-1, keepdims=True) + RMS_EPS)
    return (y * g.astype(jnp.float32)).astype(x.dtype)


def layernorm(x, g, b):
    xf = x.astype(jnp.float32)
    mu = jnp.mean(xf, axis=-1, keepdims=True)
    var = jnp.mean(jnp.square(xf - mu), axis=-1, keepdims=True)
    y = (xf - mu) * lax.rsqrt(var + LN_EPS)
    return (y * g.astype(jnp.float32) + b.astype(jnp.float32)).astype(x.dtype)


def rope_tables(positions):
    inv_freq = ROPE_THETA ** (-jnp.arange(0, QK_ROPE, 2, dtype=jnp.float32) / QK_ROPE)
    ang = positions.astype(jnp.float32)[..., None] * inv_freq
    return jnp.cos(ang), jnp.sin(ang)


def apply_rope(x, cos, sin):
    xf = x.astype(jnp.float32)
    x1, x2 = jnp.split(xf, 2, axis=-1)
    out = jnp.concatenate([x1 * cos - x2 * sin, x2 * cos + x1 * sin], axis=-1)
    return out.astype(x.dtype)


def causal_block_attention(q, k, v):
    B, S, H, D = q.shape
    nb = S // Q_BLOCK
    scale = 1.0 / math.sqrt(D)
    kf = k.astype(jnp.float32)
    kpos = jnp.arange(S)
    qb = q.reshape(B, nb, Q_BLOCK, H, D).transpose(1, 0, 2, 3, 4)

    def one_block(args):
        qi, i = args
        qpos = i * Q_BLOCK + jnp.arange(Q_BLOCK)
        s = jnp.einsum('bqhd,bkhd->bhqk', qi.astype(jnp.float32), kf) * scale
        mask = kpos[None, :] <= qpos[:, None]
        s = jnp.where(mask[None, None], s, -jnp.inf)
        p = jax.nn.softmax(s, axis=-1).astype(v.dtype)
        return jnp.einsum('bhqk,bkhd->bqhd', p, v)

    out = lax.map(one_block, (qb, jnp.arange(nb)))
    return out.transpose(1, 0, 2, 3, 4).reshape(B, S, H, v.shape[-1])


def mla_mixer(h, cos, sin, w_in, q_norm_g, w_qb, kv_norm_g, w_kvb, w_out):
    B, S, _ = h.shape
    z = h @ w_in
    q_lat, kv_lat, k_pe, gate = jnp.split(
        z, [Q_LORA, Q_LORA + KV_LORA, Q_LORA + KV_LORA + QK_ROPE], axis=-1)
    q = (rmsnorm(q_lat, q_norm_g) @ w_qb).reshape(B, S, N_HEADS, QK_NOPE + QK_ROPE)
    q_nope, q_pe = jnp.split(q, [QK_NOPE], axis=-1)
    kv = (rmsnorm(kv_lat, kv_norm_g) @ w_kvb).reshape(B, S, N_HEADS, QK_NOPE + V_HEAD)
    k_nope, v = jnp.split(kv, [QK_NOPE], axis=-1)
    q_pe = apply_rope(q_pe, cos[:, :, None, :], sin[:, :, None, :])
    k_pe = apply_rope(k_pe, cos, sin)
    q = jnp.concatenate([q_nope, q_pe], axis=-1)
    k = jnp.concatenate(
        [k_nope, jnp.broadcast_to(k_pe[:, :, None, :], (B, S, N_HEADS, QK_ROPE))], axis=-1)
    o = causal_block_attention(q, k, v).reshape(B, S, ATTN_WIDTH)
    return (o * jax.nn.silu(gate)) @ w_out


def conv_mixer(h, w_in, b_in, dw_w, dw_b, ln_g, ln_b, w_out, b_out):
    z = h @ w_in + b_in
    a, b, gate = jnp.split(z, 3, axis=-1)
    u = a * jax.nn.sigmoid(b)
    u = lax.conv_general_dilated(
        u, dw_w[:, None, :], window_strides=(1,), padding=[(CONV_K - 1, 0)],
        dimension_numbers=('NWC', 'WIO', 'NWC'),
        feature_group_count=CONV_WIDTH) + dw_b
    u = jax.nn.silu(layernorm(u, ln_g, ln_b))
    return (u * jax.nn.silu(gate)) @ w_out + b_out


def setup_inputs(seed: int = 0) -> dict:
    key = jax.random.key(seed)
    ks = jax.random.split(key, 24)
    f32 = jnp.float32

    def w(k, shape, fan_in):
        return jax.random.normal(k, shape, f32) * (fan_in ** -0.5)

    def gain(k, shape):
        return 1.0 + 0.02 * jax.random.normal(k, shape, f32)

    def bias(k, shape):
        return 0.02 * jax.random.normal(k, shape, f32)

    x = jax.random.normal(ks[0], (BATCH, SEQ, D_MODEL), f32)
    offset = jax.random.randint(ks[1], (BATCH, 1), 0, 1024, dtype=jnp.int32)
    positions = offset + jnp.arange(SEQ, dtype=jnp.int32)[None, :]
    return {
        "x": x,
        "positions": positions,
        "final_norm_g": gain(ks[2], (D_MODEL,)),
        "mla_norm_g": gain(ks[3], (N_MLA, D_MODEL)),
        "mla_w_in": w(ks[4], (N_MLA, D_MODEL, MLA_IN), D_MODEL),
        "mla_q_norm_g": gain(ks[5], (N_MLA, Q_LORA)),
        "mla_w_qb": w(ks[6], (N_MLA, Q_LORA, N_HEADS * (QK_NOPE + QK_ROPE)), Q_LORA),
        "mla_kv_norm_g": gain(ks[7], (N_MLA, KV_LORA)),
        "mla_w_kvb": w(ks[8], (N_MLA, KV_LORA, N_HEADS * (QK_NOPE + V_HEAD)), KV_LORA),
        "mla_w_out": w(ks[9], (N_MLA, ATTN_WIDTH, D_MODEL), ATTN_WIDTH),
        "conv_norm_g": gain(ks[10], (N_CONV, D_MODEL)),
        "conv_w_in": w(ks[11], (N_CONV, D_MODEL, 3 * CONV_WIDTH), D_MODEL),
        "conv_b_in": bias(ks[12], (N_CONV, 3 * CONV_WIDTH)),
        "conv_dw_w": w(ks[13], (N_CONV, CONV_K, CONV_WIDTH), CONV_K),
        "conv_dw_b": bias(ks[14], (N_CONV, CONV_WIDTH)),
        "conv_ln_g": gain(ks[15], (N_CONV, CONV_WIDTH)),
        "conv_ln_b": bias(ks[16], (N_CONV, CONV_WIDTH)),
        "conv_w_out": w(ks[17], (N_CONV, CONV_WIDTH, D_MODEL), CONV_WIDTH),
        "conv_b_out": bias(ks[18], (N_CONV, D_MODEL)),
    }


def reference(x, positions, final_norm_g,
              mla_norm_g, mla_w_in, mla_q_norm_g, mla_w_qb, mla_kv_norm_g, mla_w_kvb, mla_w_out,
              conv_norm_g, conv_w_in, conv_b_in, conv_dw_w, conv_dw_b, conv_ln_g, conv_ln_b,
              conv_w_out, conv_b_out):
    cos, sin = rope_tables(positions)
    for i in range(DEPTH):
        j = i // N_MIXERS
        if i % N_MIXERS == 0:
            h = rmsnorm(x, mla_norm_g[j])
            x = x + mla_mixer(h, cos, sin, mla_w_in[j], mla_q_norm_g[j], mla_w_qb[j],
                              mla_kv_norm_g[j], mla_w_kvb[j], mla_w_out[j])
        else:
            h = rmsnorm(x, conv_norm_g[j])
            x = x + conv_mixer(h, conv_w_in[j], conv_b_in[j], conv_dw_w[j], conv_dw_b[j],
                               conv_ln_g[j], conv_ln_b[j], conv_w_out[j], conv_b_out[j])
    return rmsnorm(x, final_norm_g)
```

```python
import functools
import math

import jax
import jax.numpy as jnp
from jax import lax
from jax.experimental import pallas as pl
from jax.experimental.pallas import tpu as pltpu

D_MODEL = 1024
N_HEADS = 8
QK_NOPE = 128
QK_ROPE = 64
V_HEAD = 128
Q_LORA = 384
KV_LORA = 256
ATTN_WIDTH = N_HEADS * V_HEAD
ROPE_THETA = 10000.0
CONV_K = 31
RMS_EPS = 1e-6
LN_EPS = 1e-5

LANES = 128
HEAD_PAD = 2 * LANES
HALF_ROPE = QK_ROPE // 2
Z_MLA = Q_LORA + KV_LORA + LANES + ATTN_WIDTH
CONV_HALO = 32
VMEM_LIMIT = 56 * 1024 * 1024


def _tile_config():
    return dict(rope_tm=512, proj_tm=256, attn_tq=256, conv_tm=256)


def _rms(x, g):
    return x * lax.rsqrt(jnp.mean(x * x, axis=-1, keepdims=True) + RMS_EPS) * g


def _rope_kernel(pos_ref, freq_ref, cmask_ref, smask_ref, c_ref, s_ref):
    ang = pos_ref[...].astype(jnp.float32) * freq_ref[...]
    c_ref[...] = jnp.cos(ang) * cmask_ref[...]
    s_ref[...] = jnp.sin(ang) * smask_ref[...]


def _rope_tables(positions, tm):
    t = positions.size
    inv_freq = ROPE_THETA ** (-jnp.arange(0, QK_ROPE, 2, dtype=jnp.float32) / QK_ROPE)
    zeros = jnp.zeros((2 * HALF_ROPE,), jnp.float32)
    ones = jnp.ones((HALF_ROPE,), jnp.float32)
    freq = jnp.concatenate([inv_freq, inv_freq, zeros])[None, :]
    cmask = jnp.concatenate([ones, ones, zeros])[None, :]
    smask = jnp.concatenate([-ones, ones, zeros])[None, :]
    row = pl.BlockSpec((1, LANES), lambda i: (0, 0))
    tile = pl.BlockSpec((tm, LANES), lambda i: (i, 0))
    return pl.pallas_call(
        _rope_kernel,
        grid=(t // tm,),
        in_specs=[pl.BlockSpec((tm, 1), lambda i: (i, 0)), row, row, row],
        out_specs=[tile, tile],
        out_shape=[jax.ShapeDtypeStruct((t, LANES), jnp.float32)] * 2,
        name="rope_tables",
    )(positions.reshape(t, 1), freq, cmask, smask)


def _rope(x, c, s):
    return x * c + pltpu.roll(x, HALF_ROPE, 1) * s


def _mla_proj_kernel(x_ref, g_ref, win_ref, gq_ref, wqb_ref, gkv_ref, wkvb_ref, c_ref, s_ref,
                     q_ref, k_ref, v_ref, gs_ref):
    h = _rms(x_ref[...], g_ref[...]).astype(jnp.bfloat16)
    z = jnp.dot(h, win_ref[...], preferred_element_type=jnp.float32)
    qn = _rms(z[:, :Q_LORA], gq_ref[...]).astype(jnp.bfloat16)
    kvn = _rms(z[:, Q_LORA:Q_LORA + KV_LORA], gkv_ref[...]).astype(jnp.bfloat16)
    q = jnp.dot(qn, wqb_ref[...], preferred_element_type=jnp.float32)
    kv = jnp.dot(kvn, wkvb_ref[...], preferred_element_type=jnp.float32)
    c = c_ref[...]
    s = s_ref[...]
    k_pe = _rope(z[:, Q_LORA + KV_LORA:Q_LORA + KV_LORA + LANES], c, s).astype(jnp.bfloat16)
    gate = z[:, Q_LORA + KV_LORA + LANES:]
    gs_ref[...] = (gate * jax.nn.sigmoid(gate)).astype(jnp.bfloat16)
    scale = 1.0 / math.sqrt(QK_NOPE + QK_ROPE)
    for hd in range(N_HEADS):
        lo = hd * HEAD_PAD
        q_ref[:, lo:lo + LANES] = (q[:, lo:lo + LANES] * scale).astype(jnp.bfloat16)
        q_ref[:, lo + LANES:lo + HEAD_PAD] = (
            _rope(q[:, lo + LANES:lo + HEAD_PAD], c, s) * scale).astype(jnp.bfloat16)
        k_ref[:, lo:lo + LANES] = kv[:, hd * QK_NOPE:(hd + 1) * QK_NOPE].astype(jnp.bfloat16)
        k_ref[:, lo + LANES:lo + HEAD_PAD] = k_pe
    v_ref[...] = kv[:, N_HEADS * QK_NOPE:].astype(jnp.bfloat16)


def _mla_proj(x2, g, win, gq, wqb, gkv, wkvb, c_tab, s_tab, tm):
    t = x2.shape[0]
    full = lambda a: pl.BlockSpec(a.shape, lambda i: (0, 0))
    tile = lambda w: pl.BlockSpec((tm, w), lambda i: (i, 0))
    return pl.pallas_call(
        _mla_proj_kernel,
        grid=(t // tm,),
        in_specs=[tile(D_MODEL), full(g), full(win), full(gq), full(wqb), full(gkv), full(wkvb),
                  tile(LANES), tile(LANES)],
        out_specs=[tile(N_HEADS * HEAD_PAD), tile(N_HEADS * HEAD_PAD), tile(ATTN_WIDTH),
                   tile(ATTN_WIDTH)],
        out_shape=[jax.ShapeDtypeStruct((t, N_HEADS * HEAD_PAD), jnp.bfloat16),
                   jax.ShapeDtypeStruct((t, N_HEADS * HEAD_PAD), jnp.bfloat16),
                   jax.ShapeDtypeStruct((t, ATTN_WIDTH), jnp.bfloat16),
                   jax.ShapeDtypeStruct((t, ATTN_WIDTH), jnp.bfloat16)],
        compiler_params=pltpu.CompilerParams(
            dimension_semantics=("arbitrary",), vmem_limit_bytes=VMEM_LIMIT),
        name="mla_proj",
    )(x2, g, win, gq, wqb, gkv, wkvb, c_tab, s_tab)


def _attn_kernel(q_ref, k_ref, v_ref, gs_ref, x_ref, wout_ref, o_ref, og_ref, *, tq):
    qi = pl.program_id(1)
    row = lax.broadcasted_iota(jnp.int32, (tq, tq), 0)
    col = lax.broadcasted_iota(jnp.int32, (tq, tq), 1)
    causal = col <= row

    for hd in range(N_HEADS):
        qh = q_ref[:, hd * HEAD_PAD:(hd + 1) * HEAD_PAD]

        def scores(j, hd=hd, qh=qh):
            start = pl.multiple_of(j * tq, tq)
            kb = k_ref[pl.ds(start, tq), hd * HEAD_PAD:(hd + 1) * HEAD_PAD]
            vb = v_ref[pl.ds(start, tq), hd * V_HEAD:(hd + 1) * V_HEAD]
            s = lax.dot_general(qh, kb, (((1,), (1,)), ((), ())),
                                preferred_element_type=jnp.float32)
            return s, vb

        s, vb = scores(qi)
        s = jnp.where(causal, s, -jnp.inf)
        m0 = jnp.max(s, axis=-1, keepdims=True)
        p = jnp.exp(s - m0)
        l0 = jnp.sum(p, axis=-1, keepdims=True)
        acc0 = jnp.dot(p.astype(jnp.bfloat16), vb, preferred_element_type=jnp.float32)

        def body(j, carry, scores=scores):
            m, l, acc = carry
            s, vb = scores(j)
            m_new = jnp.maximum(m, jnp.max(s, axis=-1, keepdims=True))
            alpha = jnp.exp(m - m_new)
            p = jnp.exp(s - m_new)
            l = alpha * l + jnp.sum(p, axis=-1, keepdims=True)
            acc = alpha * acc + jnp.dot(p.astype(jnp.bfloat16), vb,
                                        preferred_element_type=jnp.float32)
            return m_new, l, acc

        _, l, acc = lax.fori_loop(0, qi, body, (m0, l0, acc0))
        o = acc / l
        gsh = gs_ref[:, hd * V_HEAD:(hd + 1) * V_HEAD].astype(jnp.float32)
        og_ref[:, hd * V_HEAD:(hd + 1) * V_HEAD] = (o * gsh).astype(jnp.bfloat16)

    o_ref[...] = x_ref[...] + jnp.dot(og_ref[...], wout_ref[...],
                                      preferred_element_type=jnp.float32)


def _attention(q, k, v, gs, x2, wout, batch, seq, tq):
    nq = seq // tq
    qtile = lambda w: pl.BlockSpec((tq, w), lambda b, i: (b * nq + i, 0))
    per_batch = lambda w: pl.BlockSpec((seq, w), lambda b, i: (b, 0))
    return pl.pallas_call(
        functools.partial(_attn_kernel, tq=tq),
        grid=(batch, nq),
        in_specs=[qtile(N_HEADS * HEAD_PAD), per_batch(N_HEADS * HEAD_PAD), per_batch(ATTN_WIDTH),
                  qtile(ATTN_WIDTH), qtile(D_MODEL),
                  pl.BlockSpec(wout.shape, lambda b, i: (0, 0))],
        out_specs=qtile(D_MODEL),
        out_shape=jax.ShapeDtypeStruct(x2.shape, jnp.float32),
        scratch_shapes=[pltpu.VMEM((tq, ATTN_WIDTH), jnp.bfloat16)],
        compiler_params=pltpu.CompilerParams(
            dimension_semantics=("arbitrary", "arbitrary"), vmem_limit_bytes=VMEM_LIMIT),
        name="mla_attn",
    )(q, k, v, gs, x2, wout)


def _conv_kernel(x_ref, g_ref, win_ref, bin_ref, dw_ref, dwb_ref, lng_ref, lnb_ref, wout_ref,
                 bout_ref, fg_ref, o_ref, u_ref, c_ref, y_ref, *, tm, final_norm):
    nchunk = D_MODEL // LANES
    rows = 64

    @pl.when(pl.program_id(1) == 0)
    def _():
        u_ref[:, 0:CONV_HALO, :] = jnp.zeros((nchunk, CONV_HALO, LANES), jnp.float32)

    x = x_ref[...]
    h = _rms(x, g_ref[...]).astype(jnp.bfloat16)
    z = jnp.dot(h, win_ref[...], preferred_element_type=jnp.float32) + bin_ref[...]
    u = z[:, :D_MODEL] * jax.nn.sigmoid(z[:, D_MODEL:2 * D_MODEL])
    for c in range(nchunk):
        u_ref[c, CONV_HALO:CONV_HALO + tm, :] = u[:, c * LANES:(c + 1) * LANES]

    def conv_chunk(c, carry):
        w = dw_ref[c]
        for r in range(tm // rows):
            acc = jnp.zeros((rows, LANES), jnp.float32)
            for k in range(CONV_K):
                lo = CONV_HALO - (CONV_K - 1) + k + r * rows
                acc = acc + u_ref[c, lo:lo + rows, :] * w[k:k + 1, :]
            c_ref[c, r * rows:(r + 1) * rows, :] = acc
        u_ref[c, 0:CONV_HALO, :] = u_ref[c, tm:tm + CONV_HALO, :]
        return carry

    lax.fori_loop(0, nchunk, conv_chunk, 0)

    cv = jnp.concatenate([c_ref[c] for c in range(nchunk)], axis=-1) + dwb_ref[...]
    mu = jnp.mean(cv, axis=-1, keepdims=True)
    d = cv - mu
    var = jnp.mean(d * d, axis=-1, keepdims=True)
    yn = d * lax.rsqrt(var + LN_EPS) * lng_ref[...] + lnb_ref[...]
    gate = z[:, 2 * D_MODEL:]
    y = (yn * jax.nn.sigmoid(yn)) * (gate * jax.nn.sigmoid(gate))
    y_ref[...] = y.astype(jnp.bfloat16)
    out = x + jnp.dot(y_ref[...], wout_ref[...], preferred_element_type=jnp.float32) + bout_ref[...]
    if final_norm:
        out = _rms(out, fg_ref[...])
    o_ref[...] = out


def _conv_layer(x2, g, win, b_in, dw3, dw_b, ln_g, ln_b, wout, b_out, fg, batch, seq, tm,
                final_norm):
    ns = seq // tm
    nchunk = D_MODEL // LANES
    tile = pl.BlockSpec((tm, D_MODEL), lambda b, i: (b * ns + i, 0))
    full = lambda a: pl.BlockSpec(a.shape, lambda b, i: (0,) * a.ndim)
    args = (x2, g, win, b_in, dw3, dw_b, ln_g, ln_b, wout, b_out, fg)
    return pl.pallas_call(
        functools.partial(_conv_kernel, tm=tm, final_norm=final_norm),
        grid=(batch, ns),
        in_specs=[tile] + [full(a) for a in args[1:]],
        out_specs=tile,
        out_shape=jax.ShapeDtypeStruct(x2.shape, jnp.float32),
        scratch_shapes=[pltpu.VMEM((nchunk, tm + CONV_HALO, LANES), jnp.float32),
                        pltpu.VMEM((nchunk, tm, LANES), jnp.float32),
                        pltpu.VMEM((tm, D_MODEL), jnp.bfloat16)],
        compiler_params=pltpu.CompilerParams(
            dimension_semantics=("arbitrary", "arbitrary"), vmem_limit_bytes=VMEM_LIMIT),
        name="conv_layer",
    )(*args)


def _mla_weights(w_in, w_qb, w_kvb, w_out):
    bf = jnp.bfloat16
    k_pe = w_in[:, Q_LORA + KV_LORA:Q_LORA + KV_LORA + QK_ROPE]
    win = jnp.concatenate([w_in[:, :Q_LORA + KV_LORA], k_pe, k_pe,
                           w_in[:, Q_LORA + KV_LORA + QK_ROPE:]], axis=1).astype(bf)
    wq = w_qb.reshape(Q_LORA, N_HEADS, QK_NOPE + QK_ROPE)
    wqb = jnp.concatenate([wq[..., :QK_NOPE], wq[..., QK_NOPE:], wq[..., QK_NOPE:]],
                          axis=-1).reshape(Q_LORA, N_HEADS * HEAD_PAD).astype(bf)
    wkv = w_kvb.reshape(KV_LORA, N_HEADS, QK_NOPE + V_HEAD)
    wkvb = jnp.concatenate([wkv[..., :QK_NOPE].reshape(KV_LORA, -1),
                            wkv[..., QK_NOPE:].reshape(KV_LORA, -1)], axis=1).astype(bf)
    return win, wqb, wkvb, w_out.astype(bf)


def kernel(x, positions, final_norm_g, mla_norm_g, mla_w_in, mla_q_norm_g, mla_w_qb, mla_kv_norm_g, mla_w_kvb, mla_w_out, conv_norm_g, conv_w_in, conv_b_in, conv_dw_w, conv_dw_b, conv_ln_g, conv_ln_b, conv_w_out, conv_b_out):
    cfg = _tile_config()
    batch, seq, d = x.shape
    assert d == D_MODEL and seq % max(cfg.values()) == 0
    depth = mla_norm_g.shape[0] + conv_norm_g.shape[0]
    nchunk = D_MODEL // LANES
    row = lambda a: a.reshape(1, -1).astype(jnp.float32)

    c_tab, s_tab = _rope_tables(positions, cfg["rope_tm"])
    x2 = x.reshape(batch * seq, D_MODEL)
    for i in range(depth):
        j = i // 2
        if i % 2 == 0:
            win, wqb, wkvb, wout = _mla_weights(mla_w_in[j], mla_w_qb[j], mla_w_kvb[j], mla_w_out[j])
            q, k, v, gs = _mla_proj(x2, row(mla_norm_g[j]), win, row(mla_q_norm_g[j]), wqb,
                                    row(mla_kv_norm_g[j]), wkvb, c_tab, s_tab, cfg["proj_tm"])
            x2 = _attention(q, k, v, gs, x2, wout, batch, seq, cfg["attn_tq"])
        else:
            dw = jnp.pad(conv_dw_w[j], ((0, CONV_HALO - CONV_K), (0, 0)))
            dw3 = dw.reshape(CONV_HALO, nchunk, LANES).transpose(1, 0, 2)
            x2 = _conv_layer(x2, row(conv_norm_g[j]), conv_w_in[j].astype(jnp.bfloat16),
                             row(conv_b_in[j]), dw3, row(conv_dw_b[j]), row(conv_ln_g[j]),
                             row(conv_ln_b[j]), conv_w_out[j].astype(jnp.bfloat16),
                             row(conv_b_out[j]), row(final_norm_g), batch, seq, cfg["conv_tm"],
                             final_norm=(i == depth - 1))
    return x2.reshape(batch, seq, D_MODEL)
```

```python
import functools
import math

import jax
import jax.numpy as jnp
from jax import lax
from jax.experimental import pallas as pl
from jax.experimental.pallas import tpu as pltpu

D_MODEL = 1024
N_HEADS = 8
QK_NOPE = 128
QK_ROPE = 64
V_HEAD = 128
Q_LORA = 384
KV_LORA = 256
ATTN_WIDTH = N_HEADS * V_HEAD
ROPE_THETA = 10000.0
CONV_K = 31
RMS_EPS = 1e-6
LN_EPS = 1e-5

LANES = 128
HEAD_PAD = 2 * LANES
HALF_ROPE = QK_ROPE // 2
CONV_HALO = 32
VMEM_LIMIT = 56 * 1024 * 1024


def _tile_config():
    return dict(rope_tm=512, proj_tm=256, attn_tq=256, conv_tm=256)


def _rms(x, g):
    return x * lax.rsqrt(jnp.mean(x * x, axis=-1, keepdims=True) + RMS_EPS) * g


def _rope_kernel(pos_ref, freq_ref, cmask_ref, smask_ref, c_ref, s_ref):
    ang = pos_ref[...].astype(jnp.float32) * freq_ref[...]
    c_ref[...] = jnp.cos(ang) * cmask_ref[...]
    s_ref[...] = jnp.sin(ang) * smask_ref[...]


def _rope_tables(positions, tm):
    t = positions.size
    inv_freq = ROPE_THETA ** (-jnp.arange(0, QK_ROPE, 2, dtype=jnp.float32) / QK_ROPE)
    zeros = jnp.zeros((2 * HALF_ROPE,), jnp.float32)
    ones = jnp.ones((HALF_ROPE,), jnp.float32)
    freq = jnp.concatenate([inv_freq, inv_freq, zeros])[None, :]
    cmask = jnp.concatenate([ones, ones, zeros])[None, :]
    smask = jnp.concatenate([-ones, ones, zeros])[None, :]
    row = pl.BlockSpec((1, LANES), lambda i: (0, 0))
    tile = pl.BlockSpec((tm, LANES), lambda i: (i, 0))
    return pl.pallas_call(
        _rope_kernel,
        grid=(t // tm,),
        in_specs=[pl.BlockSpec((tm, 1), lambda i: (i, 0)), row, row, row],
        out_specs=[tile, tile],
        out_shape=[jax.ShapeDtypeStruct((t, LANES), jnp.float32)] * 2,
        name="rope_tables",
    )(positions.reshape(t, 1), freq, cmask, smask)


def _rope(x, c, s):
    return x * c + pltpu.roll(x, HALF_ROPE, 1) * s


def _mla_proj_kernel(x_ref, g_ref, win_ref, gq_ref, wqb_ref, gkv_ref, wkt_ref, wv_ref, c_ref, s_ref,
                     q_ref, kt_ref, va_ref, gs_ref):
    tm = x_ref.shape[0]
    h = _rms(x_ref[...], g_ref[...]).astype(jnp.bfloat16)
    z = jnp.dot(h, win_ref[...], preferred_element_type=jnp.float32)
    qn = _rms(z[:, :Q_LORA], gq_ref[...]).astype(jnp.bfloat16)
    kvn = _rms(z[:, Q_LORA:Q_LORA + KV_LORA], gkv_ref[...])
    q = jnp.dot(qn, wqb_ref[...], preferred_element_type=jnp.float32)
    v = jnp.dot(kvn.astype(jnp.bfloat16), wv_ref[...], preferred_element_type=jnp.float32)
    k_t = jnp.dot(wkt_ref[...], kvn.T.astype(jnp.bfloat16),
                  preferred_element_type=jnp.float32)
    c = c_ref[...]
    s = s_ref[...]
    kpe_t = _rope(z[:, Q_LORA + KV_LORA:Q_LORA + KV_LORA + LANES], c, s).T.astype(jnp.bfloat16)
    gate = z[:, Q_LORA + KV_LORA + LANES:]
    gs_ref[...] = (gate * jax.nn.sigmoid(gate)).astype(jnp.bfloat16)
    scale = math.log2(math.e) / math.sqrt(QK_NOPE + QK_ROPE)
    ones = jnp.ones((tm, LANES), jnp.bfloat16)
    for hd in range(N_HEADS):
        lo = hd * HEAD_PAD
        q_ref[:, lo:lo + LANES] = (q[:, lo:lo + LANES] * scale).astype(jnp.bfloat16)
        q_ref[:, lo + LANES:lo + HEAD_PAD] = (
            _rope(q[:, lo + LANES:lo + HEAD_PAD], c, s) * scale).astype(jnp.bfloat16)
        kt_ref[lo:lo + LANES, :] = k_t[hd * QK_NOPE:(hd + 1) * QK_NOPE, :].astype(jnp.bfloat16)
        kt_ref[lo + LANES:lo + HEAD_PAD, :] = kpe_t
        va_ref[:, lo:lo + LANES] = v[:, hd * V_HEAD:(hd + 1) * V_HEAD].astype(jnp.bfloat16)
        va_ref[:, lo + LANES:lo + HEAD_PAD] = ones


def _mla_proj(x2, g, win, gq, wqb, gkv, wkt, wv, c_tab, s_tab, batch, seq, tm):
    t = x2.shape[0]
    ns = seq // tm
    width = N_HEADS * HEAD_PAD
    full = lambda a: pl.BlockSpec(a.shape, lambda b, i: (0, 0))
    tile = lambda w: pl.BlockSpec((tm, w), lambda b, i: (b * ns + i, 0))
    return pl.pallas_call(
        _mla_proj_kernel,
        grid=(batch, ns),
        in_specs=[tile(D_MODEL), full(g), full(win), full(gq), full(wqb), full(gkv), full(wkt),
                  full(wv), tile(LANES), tile(LANES)],
        out_specs=[tile(width), pl.BlockSpec((width, tm), lambda b, i: (b, i)), tile(width),
                   tile(ATTN_WIDTH)],
        out_shape=[jax.ShapeDtypeStruct((t, width), jnp.bfloat16),
                   jax.ShapeDtypeStruct((batch * width, seq), jnp.bfloat16),
                   jax.ShapeDtypeStruct((t, width), jnp.bfloat16),
                   jax.ShapeDtypeStruct((t, ATTN_WIDTH), jnp.bfloat16)],
        compiler_params=pltpu.CompilerParams(
            dimension_semantics=("arbitrary", "arbitrary"), vmem_limit_bytes=VMEM_LIMIT),
        name="mla_proj",
    )(x2, g, win, gq, wqb, gkv, wkt, wv, c_tab, s_tab)


def _attn_head(qh, kt_ref, va_ref, hd, c, tq, causal):
    rows = slice(hd * HEAD_PAD, (hd + 1) * HEAD_PAD)
    past = c * tq
    s_d = jnp.dot(qh, kt_ref[rows, past:past + tq], preferred_element_type=jnp.float32)
    s_d = jnp.where(causal, s_d, -jnp.inf)
    m = jnp.max(s_d, axis=-1, keepdims=True)
    if past:
        s_p = jnp.dot(qh, kt_ref[rows, 0:past], preferred_element_type=jnp.float32)
        m = jnp.maximum(m, jnp.max(s_p, axis=-1, keepdims=True))
        p_p = jnp.exp2(s_p - m).astype(jnp.bfloat16)
    p_d = jnp.exp2(s_d - m).astype(jnp.bfloat16)
    pv = jnp.dot(p_d, va_ref[past:past + tq, rows], preferred_element_type=jnp.float32)
    if past:
        pv = pv + jnp.dot(p_p, va_ref[0:past, rows], preferred_element_type=jnp.float32)
    return pv[:, :V_HEAD] / pv[:, V_HEAD:]


def _attn_kernel(q_ref, kt_ref, va_ref, gs_ref, x_ref, wout_ref, o_ref, og_ref, *, tq, nq):
    qi = pl.program_id(1)
    row = lax.broadcasted_iota(jnp.int32, (tq, tq), 0)
    col = lax.broadcasted_iota(jnp.int32, (tq, tq), 1)
    causal = col <= row

    for c in range(nq):
        @pl.when(qi == c)
        def _(c=c):
            for hd in range(N_HEADS):
                qh = q_ref[:, hd * HEAD_PAD:(hd + 1) * HEAD_PAD]
                o = _attn_head(qh, kt_ref, va_ref, hd, c, tq, causal)
                gsh = gs_ref[:, hd * V_HEAD:(hd + 1) * V_HEAD].astype(jnp.float32)
                og_ref[:, hd * V_HEAD:(hd + 1) * V_HEAD] = (o * gsh).astype(jnp.bfloat16)

    o_ref[...] = x_ref[...] + jnp.dot(og_ref[...], wout_ref[...],
                                      preferred_element_type=jnp.float32)


def _attention(q, kt, va, gs, x2, wout, batch, seq, tq):
    nq = seq // tq
    width = N_HEADS * HEAD_PAD
    qtile = lambda w: pl.BlockSpec((tq, w), lambda b, i: (b * nq + i, 0))
    return pl.pallas_call(
        functools.partial(_attn_kernel, tq=tq, nq=nq),
        grid=(batch, nq),
        in_specs=[qtile(width),
                  pl.BlockSpec((width, seq), lambda b, i: (b, 0)),
                  pl.BlockSpec((seq, width), lambda b, i: (b, 0)),
                  qtile(ATTN_WIDTH), qtile(D_MODEL),
                  pl.BlockSpec(wout.shape, lambda b, i: (0, 0))],
        out_specs=qtile(D_MODEL),
        out_shape=jax.ShapeDtypeStruct(x2.shape, jnp.float32),
        scratch_shapes=[pltpu.VMEM((tq, ATTN_WIDTH), jnp.bfloat16)],
        compiler_params=pltpu.CompilerParams(
            dimension_semantics=("arbitrary", "arbitrary"), vmem_limit_bytes=VMEM_LIMIT),
        name="mla_attn",
    )(q, kt, va, gs, x2, wout)


def _conv_kernel(x_ref, g_ref, win_ref, bin_ref, dw_ref, dwb_ref, lng_ref, lnb_ref, wout_ref,
                 bout_ref, fg_ref, o_ref, u_ref, c_ref, y_ref, *, tm, final_norm):
    nchunk = D_MODEL // LANES
    rows = 64

    @pl.when(pl.program_id(1) == 0)
    def _():
        u_ref[:, 0:CONV_HALO, :] = jnp.zeros((nchunk, CONV_HALO, LANES), jnp.float32)

    x = x_ref[...]
    h = _rms(x, g_ref[...]).astype(jnp.bfloat16)

    def proj(lo, width):
        return (jnp.dot(h, win_ref[:, lo:lo + width], preferred_element_type=jnp.float32)
                + bin_ref[:, lo:lo + width])

    group = 2 * LANES
    for gi in range(D_MODEL // group):
        u = proj(gi * group, group) * jax.nn.sigmoid(proj(D_MODEL + gi * group, group))
        for half in range(group // LANES):
            c = gi * (group // LANES) + half
            u_ref[c, CONV_HALO:CONV_HALO + tm, :] = u[:, half * LANES:(half + 1) * LANES]
            w = dw_ref[c]
            for r in range(tm // rows):
                acc = jnp.zeros((rows, LANES), jnp.float32)
                for k in range(CONV_K):
                    lo = CONV_HALO - (CONV_K - 1) + k + r * rows
                    acc = acc + u_ref[c, lo:lo + rows, :] * w[k:k + 1, :]
                c_ref[c, r * rows:(r + 1) * rows, :] = acc
            u_ref[c, 0:CONV_HALO, :] = u_ref[c, tm:tm + CONV_HALO, :]

    cv = jnp.concatenate([c_ref[c] for c in range(nchunk)], axis=-1) + dwb_ref[...]
    mu = jnp.mean(cv, axis=-1, keepdims=True)
    d = cv - mu
    var = jnp.mean(d * d, axis=-1, keepdims=True)
    yn = d * lax.rsqrt(var + LN_EPS) * lng_ref[...] + lnb_ref[...]
    gate = proj(2 * D_MODEL, D_MODEL)
    y = (yn * jax.nn.sigmoid(yn)) * (gate * jax.nn.sigmoid(gate))
    y_ref[...] = y.astype(jnp.bfloat16)
    out = x + jnp.dot(y_ref[...], wout_ref[...], preferred_element_type=jnp.float32) + bout_ref[...]
    if final_norm:
        out = _rms(out, fg_ref[...])
    o_ref[...] = out


def _conv_layer(x2, g, win, b_in, dw3, dw_b, ln_g, ln_b, wout, b_out, fg, batch, seq, tm,
                final_norm):
    ns = seq // tm
    nchunk = D_MODEL // LANES
    tile = pl.BlockSpec((tm, D_MODEL), lambda b, i: (b * ns + i, 0))
    full = lambda a: pl.BlockSpec(a.shape, lambda b, i: (0,) * a.ndim)
    args = (x2, g, win, b_in, dw3, dw_b, ln_g, ln_b, wout, b_out, fg)
    return pl.pallas_call(
        functools.partial(_conv_kernel, tm=tm, final_norm=final_norm),
        grid=(batch, ns),
        in_specs=[tile] + [full(a) for a in args[1:]],
        out_specs=tile,
        out_shape=jax.ShapeDtypeStruct(x2.shape, jnp.float32),
        scratch_shapes=[pltpu.VMEM((nchunk, tm + CONV_HALO, LANES), jnp.float32),
                        pltpu.VMEM((nchunk, tm, LANES), jnp.float32),
                        pltpu.VMEM((tm, D_MODEL), jnp.bfloat16)],
        compiler_params=pltpu.CompilerParams(
            dimension_semantics=("arbitrary", "arbitrary"), vmem_limit_bytes=VMEM_LIMIT),
        name="conv_layer",
    )(*args)


def _mla_weights(w_in, w_qb, w_kvb, w_out):
    bf = jnp.bfloat16
    k_pe = w_in[:, Q_LORA + KV_LORA:Q_LORA + KV_LORA + QK_ROPE]
    win = jnp.concatenate([w_in[:, :Q_LORA + KV_LORA], k_pe, k_pe,
                           w_in[:, Q_LORA + KV_LORA + QK_ROPE:]], axis=1).astype(bf)
    wq = w_qb.reshape(Q_LORA, N_HEADS, QK_NOPE + QK_ROPE)
    wqb = jnp.concatenate([wq[..., :QK_NOPE], wq[..., QK_NOPE:], wq[..., QK_NOPE:]],
                          axis=-1).reshape(Q_LORA, N_HEADS * HEAD_PAD).astype(bf)
    wkv = w_kvb.reshape(KV_LORA, N_HEADS, QK_NOPE + V_HEAD)
    wkt = wkv[..., :QK_NOPE].reshape(KV_LORA, -1).T.astype(bf)
    wv = wkv[..., QK_NOPE:].reshape(KV_LORA, -1).astype(bf)
    return win, wqb, wkt, wv, w_out.astype(bf)


def kernel(x, positions, final_norm_g, mla_norm_g, mla_w_in, mla_q_norm_g, mla_w_qb, mla_kv_norm_g, mla_w_kvb, mla_w_out, conv_norm_g, conv_w_in, conv_b_in, conv_dw_w, conv_dw_b, conv_ln_g, conv_ln_b, conv_w_out, conv_b_out):
    cfg = _tile_config()
    batch, seq, d = x.shape
    depth = mla_norm_g.shape[0] + conv_norm_g.shape[0]
    assert d == D_MODEL and seq % max(cfg.values()) == 0 and depth % 2 == 0
    nchunk = D_MODEL // LANES
    row = lambda a: a.reshape(1, -1).astype(jnp.float32)

    c_tab, s_tab = _rope_tables(positions, cfg["rope_tm"])
    x2 = x.reshape(batch * seq, D_MODEL)
    for i in range(depth):
        j = i // 2
        if i % 2 == 0:
            win, wqb, wkt, wv, wout = _mla_weights(mla_w_in[j], mla_w_qb[j], mla_w_kvb[j],
                                                   mla_w_out[j])
            q, kt, va, gs = _mla_proj(x2, row(mla_norm_g[j]), win, row(mla_q_norm_g[j]), wqb,
                                      row(mla_kv_norm_g[j]), wkt, wv, c_tab, s_tab, batch, seq,
                                      cfg["proj_tm"])
            x2 = _attention(q, kt, va, gs, x2, wout, batch, seq, cfg["attn_tq"])
        else:
            dw = jnp.pad(conv_dw_w[j], ((0, CONV_HALO - CONV_K), (0, 0)))
            dw3 = dw.reshape(CONV_HALO, nchunk, LANES).transpose(1, 0, 2)
            x2 = _conv_layer(x2, row(conv_norm_g[j]), conv_w_in[j].astype(jnp.bfloat16),
                             row(conv_b_in[j]), dw3, row(conv_dw_b[j]), row(conv_ln_g[j]),
                             row(conv_ln_b[j]), conv_w_out[j].astype(jnp.bfloat16),
                             row(conv_b_out[j]), row(final_norm_g), batch, seq, cfg["conv_tm"],
                             final_norm=(i == depth - 1))
    return x2.reshape(batch, seq, D_MODEL)
```

```python
import functools
import math

import jax
import jax.numpy as jnp
from jax import lax
from jax.experimental import pallas as pl
from jax.experimental.pallas import tpu as pltpu

D_MODEL = 1024
N_HEADS = 8
QK_NOPE = 128
QK_ROPE = 64
V_HEAD = 128
Q_LORA = 384
KV_LORA = 256
ATTN_WIDTH = N_HEADS * V_HEAD
ROPE_THETA = 10000.0
CONV_K = 31
RMS_EPS = 1e-6
LN_EPS = 1e-5

LANES = 128
HEAD_PAD = 2 * LANES
HALF_ROPE = QK_ROPE // 2
CONV_HALO = 32
VMEM_LIMIT = 56 * 1024 * 1024


def _tile_config():
    return dict(rope_tm=512, proj_tm=512, attn_tq=256, conv_tm=512)


def _rms(x, g):
    return x * lax.rsqrt(jnp.mean(x * x, axis=-1, keepdims=True) + RMS_EPS) * g


def _rope_kernel(pos_ref, freq_ref, cmask_ref, smask_ref, c_ref, s_ref):
    ang = pos_ref[...].astype(jnp.float32) * freq_ref[...]
    c_ref[...] = jnp.cos(ang) * cmask_ref[...]
    s_ref[...] = jnp.sin(ang) * smask_ref[...]


def _rope_tables(positions, tm):
    t = positions.size
    inv_freq = ROPE_THETA ** (-jnp.arange(0, QK_ROPE, 2, dtype=jnp.float32) / QK_ROPE)
    zeros = jnp.zeros((2 * HALF_ROPE,), jnp.float32)
    ones = jnp.ones((HALF_ROPE,), jnp.float32)
    freq = jnp.concatenate([inv_freq, inv_freq, zeros])[None, :]
    cmask = jnp.concatenate([ones, ones, zeros])[None, :]
    smask = jnp.concatenate([-ones, ones, zeros])[None, :]
    row = pl.BlockSpec((1, LANES), lambda i: (0, 0))
    tile = pl.BlockSpec((tm, LANES), lambda i: (i, 0))
    return pl.pallas_call(
        _rope_kernel,
        grid=(t // tm,),
        in_specs=[pl.BlockSpec((tm, 1), lambda i: (i, 0)), row, row, row],
        out_specs=[tile, tile],
        out_shape=[jax.ShapeDtypeStruct((t, LANES), jnp.float32)] * 2,
        name="rope_tables",
    )(positions.reshape(t, 1), freq, cmask, smask)


def _rope(x, c, s):
    return x * c + pltpu.roll(x, HALF_ROPE, 1) * s


def _mla_proj_kernel(x_ref, g_ref, win_ref, gq_ref, wqb_ref, gkv_ref, wkt_ref, wv_ref, c_ref, s_ref,
                     q_ref, kt_ref, va_ref, gs_ref):
    tm = x_ref.shape[0]
    h = _rms(x_ref[...], g_ref[...]).astype(jnp.bfloat16)
    z = jnp.dot(h, win_ref[...], preferred_element_type=jnp.float32)
    qn = _rms(z[:, :Q_LORA], gq_ref[...]).astype(jnp.bfloat16)
    kvn = _rms(z[:, Q_LORA:Q_LORA + KV_LORA], gkv_ref[...])
    q = jnp.dot(qn, wqb_ref[...], preferred_element_type=jnp.float32)
    v = jnp.dot(kvn.astype(jnp.bfloat16), wv_ref[...], preferred_element_type=jnp.float32)
    k_t = jnp.dot(wkt_ref[...], kvn.T.astype(jnp.bfloat16),
                  preferred_element_type=jnp.float32)
    c = c_ref[...]
    s = s_ref[...]
    kpe_t = _rope(z[:, Q_LORA + KV_LORA:Q_LORA + KV_LORA + LANES], c, s).T.astype(jnp.bfloat16)
    gate = z[:, Q_LORA + KV_LORA + LANES:]
    gs_ref[...] = (gate * jax.nn.sigmoid(gate)).astype(jnp.bfloat16)
    scale = math.log2(math.e) / math.sqrt(QK_NOPE + QK_ROPE)
    ones = jnp.ones((tm, LANES), jnp.bfloat16)
    for hd in range(N_HEADS):
        lo = hd * HEAD_PAD
        q_ref[:, lo:lo + LANES] = (q[:, lo:lo + LANES] * scale).astype(jnp.bfloat16)
        q_ref[:, lo + LANES:lo + HEAD_PAD] = (
            _rope(q[:, lo + LANES:lo + HEAD_PAD], c, s) * scale).astype(jnp.bfloat16)
        kt_ref[lo:lo + LANES, :] = k_t[hd * QK_NOPE:(hd + 1) * QK_NOPE, :].astype(jnp.bfloat16)
        kt_ref[lo + LANES:lo + HEAD_PAD, :] = kpe_t
        va_ref[:, lo:lo + LANES] = v[:, hd * V_HEAD:(hd + 1) * V_HEAD].astype(jnp.bfloat16)
        va_ref[:, lo + LANES:lo + HEAD_PAD] = ones


def _mla_proj(x2, g, win, gq, wqb, gkv, wkt, wv, c_tab, s_tab, batch, seq, tm):
    t = x2.shape[0]
    ns = seq // tm
    width = N_HEADS * HEAD_PAD
    full = lambda a: pl.BlockSpec(a.shape, lambda b, i: (0, 0))
    tile = lambda w: pl.BlockSpec((tm, w), lambda b, i: (b * ns + i, 0))
    return pl.pallas_call(
        _mla_proj_kernel,
        grid=(batch, ns),
        in_specs=[tile(D_MODEL), full(g), full(win), full(gq), full(wqb), full(gkv), full(wkt),
                  full(wv), tile(LANES), tile(LANES)],
        out_specs=[tile(width), pl.BlockSpec((width, tm), lambda b, i: (b, i)), tile(width),
                   tile(ATTN_WIDTH)],
        out_shape=[jax.ShapeDtypeStruct((t, width), jnp.bfloat16),
                   jax.ShapeDtypeStruct((batch * width, seq), jnp.bfloat16),
                   jax.ShapeDtypeStruct((t, width), jnp.bfloat16),
                   jax.ShapeDtypeStruct((t, ATTN_WIDTH), jnp.bfloat16)],
        compiler_params=pltpu.CompilerParams(
            dimension_semantics=("arbitrary", "arbitrary"), vmem_limit_bytes=VMEM_LIMIT),
        name="mla_proj",
    )(x2, g, win, gq, wqb, gkv, wkt, wv, c_tab, s_tab)


def _attn_head(qh, kt_ref, va_ref, hd, c, tq, causal):
    rows = slice(hd * HEAD_PAD, (hd + 1) * HEAD_PAD)
    past = c * tq
    s_d = jnp.dot(qh, kt_ref[rows, past:past + tq], preferred_element_type=jnp.float32)
    s_d = jnp.where(causal, s_d, -jnp.inf)
    m = jnp.max(s_d, axis=-1, keepdims=True)
    if past:
        s_p = jnp.dot(qh, kt_ref[rows, 0:past], preferred_element_type=jnp.float32)
        m = jnp.maximum(m, jnp.max(s_p, axis=-1, keepdims=True))
        p_p = jnp.exp2(s_p - m).astype(jnp.bfloat16)
    p_d = jnp.exp2(s_d - m).astype(jnp.bfloat16)
    pv = jnp.dot(p_d, va_ref[past:past + tq, rows], preferred_element_type=jnp.float32)
    if past:
        pv = pv + jnp.dot(p_p, va_ref[0:past, rows], preferred_element_type=jnp.float32)
    return pv[:, :V_HEAD] / pv[:, V_HEAD:]


def _attn_kernel(q_ref, kt_ref, va_ref, gs_ref, x_ref, wout_ref, o_ref, og_ref, *, tq, nq):
    qi = pl.program_id(1)
    row = lax.broadcasted_iota(jnp.int32, (tq, tq), 0)
    col = lax.broadcasted_iota(jnp.int32, (tq, tq), 1)
    causal = col <= row

    for c in range(nq):
        @pl.when(qi == c)
        def _(c=c):
            for hd in range(N_HEADS):
                qh = q_ref[:, hd * HEAD_PAD:(hd + 1) * HEAD_PAD]
                o = _attn_head(qh, kt_ref, va_ref, hd, c, tq, causal)
                gsh = gs_ref[:, hd * V_HEAD:(hd + 1) * V_HEAD].astype(jnp.float32)
                og_ref[:, hd * V_HEAD:(hd + 1) * V_HEAD] = (o * gsh).astype(jnp.bfloat16)

    o_ref[...] = x_ref[...] + jnp.dot(og_ref[...], wout_ref[...],
                                      preferred_element_type=jnp.float32)


def _attention(q, kt, va, gs, x2, wout, batch, seq, tq):
    nq = seq // tq
    width = N_HEADS * HEAD_PAD
    qtile = lambda w: pl.BlockSpec((tq, w), lambda b, i: (b * nq + i, 0))
    return pl.pallas_call(
        functools.partial(_attn_kernel, tq=tq, nq=nq),
        grid=(batch, nq),
        in_specs=[qtile(width),
                  pl.BlockSpec((width, seq), lambda b, i: (b, 0)),
                  pl.BlockSpec((seq, width), lambda b, i: (b, 0)),
                  qtile(ATTN_WIDTH), qtile(D_MODEL),
                  pl.BlockSpec(wout.shape, lambda b, i: (0, 0))],
        out_specs=qtile(D_MODEL),
        out_shape=jax.ShapeDtypeStruct(x2.shape, jnp.float32),
        scratch_shapes=[pltpu.VMEM((tq, ATTN_WIDTH), jnp.bfloat16)],
        compiler_params=pltpu.CompilerParams(
            dimension_semantics=("arbitrary", "arbitrary"), vmem_limit_bytes=VMEM_LIMIT),
        name="mla_attn",
    )(q, kt, va, gs, x2, wout)


def _conv_kernel(x_ref, g_ref, win_ref, bin_ref, dw_ref, dwb_ref, lng_ref, lnb_ref, wout_ref,
                 bout_ref, fg_ref, o_ref, u_ref, c_ref, y_ref, *, tm, final_norm):
    nchunk = D_MODEL // LANES
    rows = 64

    @pl.when(pl.program_id(1) == 0)
    def _():
        u_ref[:, 0:CONV_HALO, :] = jnp.zeros((nchunk, CONV_HALO, LANES), jnp.float32)

    x = x_ref[...]
    h = _rms(x, g_ref[...]).astype(jnp.bfloat16)

    def proj(lo, width):
        return (jnp.dot(h, win_ref[:, lo:lo + width], preferred_element_type=jnp.float32)
                + bin_ref[:, lo:lo + width])

    group = 2 * LANES
    for gi in range(D_MODEL // group):
        u = proj(gi * group, group) * jax.nn.sigmoid(proj(D_MODEL + gi * group, group))
        for half in range(group // LANES):
            c = gi * (group // LANES) + half
            u_ref[c, CONV_HALO:CONV_HALO + tm, :] = u[:, half * LANES:(half + 1) * LANES]
            w = dw_ref[c]
            for r in range(tm // rows):
                acc = jnp.zeros((rows, LANES), jnp.float32)
                for k in range(CONV_K):
                    lo = CONV_HALO - (CONV_K - 1) + k + r * rows
                    acc = acc + u_ref[c, lo:lo + rows, :] * w[k:k + 1, :]
                c_ref[c, r * rows:(r + 1) * rows, :] = acc
            u_ref[c, 0:CONV_HALO, :] = u_ref[c, tm:tm + CONV_HALO, :]

    cv = jnp.concatenate([c_ref[c] for c in range(nchunk)], axis=-1) + dwb_ref[...]
    mu = jnp.mean(cv, axis=-1, keepdims=True)
    d = cv - mu
    var = jnp.mean(d * d, axis=-1, keepdims=True)
    yn = d * lax.rsqrt(var + LN_EPS) * lng_ref[...] + lnb_ref[...]
    gate = proj(2 * D_MODEL, D_MODEL)
    y = (yn * jax.nn.sigmoid(yn)) * (gate * jax.nn.sigmoid(gate))
    y_ref[...] = y.astype(jnp.bfloat16)
    out = x + jnp.dot(y_ref[...], wout_ref[...], preferred_element_type=jnp.float32) + bout_ref[...]
    if final_norm:
        out = _rms(out, fg_ref[...])
    o_ref[...] = out


def _conv_layer(x2, g, win, b_in, dw3, dw_b, ln_g, ln_b, wout, b_out, fg, batch, seq, tm,
                final_norm):
    ns = seq // tm
    nchunk = D_MODEL // LANES
    tile = pl.BlockSpec((tm, D_MODEL), lambda b, i: (b * ns + i, 0))
    full = lambda a: pl.BlockSpec(a.shape, lambda b, i: (0,) * a.ndim)
    args = (x2, g, win, b_in, dw3, dw_b, ln_g, ln_b, wout, b_out, fg)
    return pl.pallas_call(
        functools.partial(_conv_kernel, tm=tm, final_norm=final_norm),
        grid=(batch, ns),
        in_specs=[tile] + [full(a) for a in args[1:]],
        out_specs=tile,
        out_shape=jax.ShapeDtypeStruct(x2.shape, jnp.float32),
        scratch_shapes=[pltpu.VMEM((nchunk, tm + CONV_HALO, LANES), jnp.float32),
                        pltpu.VMEM((nchunk, tm, LANES), jnp.float32),
                        pltpu.VMEM((tm, D_MODEL), jnp.bfloat16)],
        compiler_params=pltpu.CompilerParams(
            dimension_semantics=("arbitrary", "arbitrary"), vmem_limit_bytes=VMEM_LIMIT),
        name="conv_layer",
    )(*args)


def _mla_weights(w_in, w_qb, w_kvb, w_out):
    bf = jnp.bfloat16
    k_pe = w_in[:, Q_LORA + KV_LORA:Q_LORA + KV_LORA + QK_ROPE]
    win = jnp.concatenate([w_in[:, :Q_LORA + KV_LORA], k_pe, k_pe,
                           w_in[:, Q_LORA + KV_LORA + QK_ROPE:]], axis=1).astype(bf)
    wq = w_qb.reshape(Q_LORA, N_HEADS, QK_NOPE + QK_ROPE)
    wqb = jnp.concatenate([wq[..., :QK_NOPE], wq[..., QK_NOPE:], wq[..., QK_NOPE:]],
                          axis=-1).reshape(Q_LORA, N_HEADS * HEAD_PAD).astype(bf)
    wkv = w_kvb.reshape(KV_LORA, N_HEADS, QK_NOPE + V_HEAD)
    wkt = wkv[..., :QK_NOPE].reshape(KV_LORA, -1).T.astype(bf)
    wv = wkv[..., QK_NOPE:].reshape(KV_LORA, -1).astype(bf)
    return win, wqb, wkt, wv, w_out.astype(bf)


def kernel(x, positions, final_norm_g, mla_norm_g, mla_w_in, mla_q_norm_g, mla_w_qb, mla_kv_norm_g, mla_w_kvb, mla_w_out, conv_norm_g, conv_w_in, conv_b_in, conv_dw_w, conv_dw_b, conv_ln_g, conv_ln_b, conv_w_out, conv_b_out):
    cfg = _tile_config()
    batch, seq, d = x.shape
    depth = mla_norm_g.shape[0] + conv_norm_g.shape[0]
    assert d == D_MODEL and seq % max(cfg.values()) == 0 and depth % 2 == 0
    nchunk = D_MODEL // LANES
    row = lambda a: a.reshape(1, -1).astype(jnp.float32)

    c_tab, s_tab = _rope_tables(positions, cfg["rope_tm"])
    x2 = x.reshape(batch * seq, D_MODEL)
    for i in range(depth):
        j = i // 2
        if i % 2 == 0:
            win, wqb, wkt, wv, wout = _mla_weights(mla_w_in[j], mla_w_qb[j], mla_w_kvb[j],
                                                   mla_w_out[j])
            q, kt, va, gs = _mla_proj(x2, row(mla_norm_g[j]), win, row(mla_q_norm_g[j]), wqb,
                                      row(mla_kv_norm_g[j]), wkt, wv, c_tab, s_tab, batch, seq,
                                      cfg["proj_tm"])
            x2 = _attention(q, kt, va, gs, x2, wout, batch, seq, cfg["attn_tq"])
        else:
            dw = jnp.pad(conv_dw_w[j], ((0, CONV_HALO - CONV_K), (0, 0)))
            dw3 = dw.reshape(CONV_HALO, nchunk, LANES).transpose(1, 0, 2)
            x2 = _conv_layer(x2, row(conv_norm_g[j]), conv_w_in[j].astype(jnp.bfloat16),
                             row(conv_b_in[j]), dw3, row(conv_dw_b[j]), row(conv_ln_g[j]),
                             row(conv_ln_b[j]), conv_w_out[j].astype(jnp.bfloat16),
                             row(conv_b_out[j]), row(final_norm_g), batch, seq, cfg["conv_tm"],
                             final_norm=(i == depth - 1))
    return x2.reshape(batch, seq, D_MODEL)
```

```python
import functools
import math

import jax
import jax.numpy as jnp
from jax import lax
from jax.experimental import pallas as pl
from jax.experimental.pallas import tpu as pltpu

D_MODEL = 1024
N_HEADS = 8
QK_NOPE = 128
QK_ROPE = 64
V_HEAD = 128
Q_LORA = 384
KV_LORA = 256
ATTN_WIDTH = N_HEADS * V_HEAD
ROPE_THETA = 10000.0
CONV_K = 31
RMS_EPS = 1e-6
LN_EPS = 1e-5

LANES = 128
HEAD_PAD = 2 * LANES
HALF_ROPE = QK_ROPE // 2
CONV_HALO = 32
VMEM_LIMIT = 56 * 1024 * 1024


def _tile_config():
    return dict(rope_tm=512, proj_tm=512, attn_tq=256, conv_tm=512)


def _rms(x, g):
    return x * lax.rsqrt(jnp.mean(x * x, axis=-1, keepdims=True) + RMS_EPS) * g


def _rope_kernel(pos_ref, freq_ref, cmask_ref, smask_ref, c_ref, s_ref):
    half = pos_ref.shape[0] // 2
    lane = lax.broadcasted_iota(jnp.int32, (half, LANES), 1)
    pos = jnp.where(lane < QK_ROPE,
                    jnp.broadcast_to(pos_ref[0:half, :], (half, LANES)),
                    jnp.broadcast_to(pos_ref[half:, :], (half, LANES)))
    ang = pos.astype(jnp.float32) * freq_ref[...]
    c = jnp.cos(ang)
    s = jnp.sin(ang)
    cmask = cmask_ref[...]
    smask = smask_ref[...]
    c_ref[0:half, :] = c * cmask
    c_ref[half:, :] = pltpu.roll(c, QK_ROPE, 1) * cmask
    s_ref[0:half, :] = s * smask
    s_ref[half:, :] = pltpu.roll(s, QK_ROPE, 1) * smask


def _rope_tables(positions, tm):
    t = positions.size
    inv_freq = ROPE_THETA ** (-jnp.arange(0, QK_ROPE, 2, dtype=jnp.float32) / QK_ROPE)
    zeros = jnp.zeros((2 * HALF_ROPE,), jnp.float32)
    ones = jnp.ones((HALF_ROPE,), jnp.float32)
    freq = jnp.concatenate([inv_freq] * 4)[None, :]
    cmask = jnp.concatenate([ones, ones, zeros])[None, :]
    smask = jnp.concatenate([-ones, ones, zeros])[None, :]
    row = pl.BlockSpec((1, LANES), lambda i: (0, 0))
    tile = pl.BlockSpec((tm, LANES), lambda i: (i, 0))
    return pl.pallas_call(
        _rope_kernel,
        grid=(t // tm,),
        in_specs=[pl.BlockSpec((tm, 1), lambda i: (i, 0)), row, row, row],
        out_specs=[tile, tile],
        out_shape=[jax.ShapeDtypeStruct((t, LANES), jnp.float32)] * 2,
        name="rope_tables",
    )(positions.reshape(t, 1), freq, cmask, smask)


def _rope(x, c, s):
    return x * c + pltpu.roll(x, HALF_ROPE, 1) * s


def _mla_proj_kernel(x_ref, g_ref, win_ref, gq_ref, wqb_ref, gkv_ref, wkt_ref, wv_ref, c_ref, s_ref,
                     q_ref, kt_ref, va_ref, gs_ref):
    tm = x_ref.shape[0]
    h = _rms(x_ref[...], g_ref[...]).astype(jnp.bfloat16)
    z = jnp.dot(h, win_ref[...], preferred_element_type=jnp.float32)
    qn = _rms(z[:, :Q_LORA], gq_ref[...]).astype(jnp.bfloat16)
    kvn = _rms(z[:, Q_LORA:Q_LORA + KV_LORA], gkv_ref[...])
    q = jnp.dot(qn, wqb_ref[...], preferred_element_type=jnp.float32)
    v = jnp.dot(kvn.astype(jnp.bfloat16), wv_ref[...], preferred_element_type=jnp.float32)
    k_t = jnp.dot(wkt_ref[...], kvn.T.astype(jnp.bfloat16),
                  preferred_element_type=jnp.float32)
    c = c_ref[...]
    s = s_ref[...]
    kpe_t = _rope(z[:, Q_LORA + KV_LORA:Q_LORA + KV_LORA + LANES], c, s).T.astype(jnp.bfloat16)
    gate = z[:, Q_LORA + KV_LORA + LANES:]
    gs_ref[...] = (gate * jax.nn.sigmoid(gate)).astype(jnp.bfloat16)
    scale = math.log2(math.e) / math.sqrt(QK_NOPE + QK_ROPE)
    ones = jnp.ones((tm, LANES), jnp.bfloat16)
    for hd in range(N_HEADS):
        lo = hd * HEAD_PAD
        q_ref[:, lo:lo + LANES] = (q[:, lo:lo + LANES] * scale).astype(jnp.bfloat16)
        q_ref[:, lo + LANES:lo + HEAD_PAD] = (
            _rope(q[:, lo + LANES:lo + HEAD_PAD], c, s) * scale).astype(jnp.bfloat16)
        kt_ref[lo:lo + LANES, :] = k_t[hd * QK_NOPE:(hd + 1) * QK_NOPE, :].astype(jnp.bfloat16)
        kt_ref[lo + LANES:lo + HEAD_PAD, :] = kpe_t
        va_ref[:, lo:lo + LANES] = v[:, hd * V_HEAD:(hd + 1) * V_HEAD].astype(jnp.bfloat16)
        va_ref[:, lo + LANES:lo + HEAD_PAD] = ones


def _mla_proj(x2, g, win, gq, wqb, gkv, wkt, wv, c_tab, s_tab, batch, seq, tm):
    t = x2.shape[0]
    ns = seq // tm
    width = N_HEADS * HEAD_PAD
    full = lambda a: pl.BlockSpec(a.shape, lambda b, i: (0, 0))
    tile = lambda w: pl.BlockSpec((tm, w), lambda b, i: (b * ns + i, 0))
    return pl.pallas_call(
        _mla_proj_kernel,
        grid=(batch, ns),
        in_specs=[tile(D_MODEL), full(g), full(win), full(gq), full(wqb), full(gkv), full(wkt),
                  full(wv), tile(LANES), tile(LANES)],
        out_specs=[tile(width), pl.BlockSpec((width, tm), lambda b, i: (b, i)), tile(width),
                   tile(ATTN_WIDTH)],
        out_shape=[jax.ShapeDtypeStruct((t, width), jnp.bfloat16),
                   jax.ShapeDtypeStruct((batch * width, seq), jnp.bfloat16),
                   jax.ShapeDtypeStruct((t, width), jnp.bfloat16),
                   jax.ShapeDtypeStruct((t, ATTN_WIDTH), jnp.bfloat16)],
        compiler_params=pltpu.CompilerParams(
            dimension_semantics=("arbitrary", "arbitrary"), vmem_limit_bytes=VMEM_LIMIT),
        name="mla_proj",
    )(x2, g, win, gq, wqb, gkv, wkt, wv, c_tab, s_tab)


def _attn_head(qh, kt_ref, va_ref, hd, c, tq, causal):
    rows = slice(hd * HEAD_PAD, (hd + 1) * HEAD_PAD)
    past = c * tq
    s_d = jnp.dot(qh, kt_ref[rows, past:past + tq], preferred_element_type=jnp.float32)
    s_d = jnp.where(causal, s_d, -jnp.inf)
    m = jnp.max(s_d, axis=-1, keepdims=True)
    if past:
        s_p = jnp.dot(qh, kt_ref[rows, 0:past], preferred_element_type=jnp.float32)
        m = jnp.maximum(m, jnp.max(s_p, axis=-1, keepdims=True))
        p_p = jnp.exp2(s_p - m).astype(jnp.bfloat16)
    p_d = jnp.exp2(s_d - m).astype(jnp.bfloat16)
    pv = jnp.dot(p_d, va_ref[past:past + tq, rows], preferred_element_type=jnp.float32)
    if past:
        pv = pv + jnp.dot(p_p, va_ref[0:past, rows], preferred_element_type=jnp.float32)
    return pv[:, :V_HEAD] / pv[:, V_HEAD:]


def _attn_kernel(q_ref, kt_ref, va_ref, gs_ref, x_ref, wout_ref, o_ref, og_ref, *, tq, nq):
    qi = pl.program_id(1)
    row = lax.broadcasted_iota(jnp.int32, (tq, tq), 0)
    col = lax.broadcasted_iota(jnp.int32, (tq, tq), 1)
    causal = col <= row

    for c in range(nq):
        @pl.when(qi == c)
        def _(c=c):
            for hd in range(N_HEADS):
                qh = q_ref[:, hd * HEAD_PAD:(hd + 1) * HEAD_PAD]
                o = _attn_head(qh, kt_ref, va_ref, hd, c, tq, causal)
                gsh = gs_ref[:, hd * V_HEAD:(hd + 1) * V_HEAD].astype(jnp.float32)
                og_ref[:, hd * V_HEAD:(hd + 1) * V_HEAD] = (o * gsh).astype(jnp.bfloat16)

    o_ref[...] = x_ref[...] + jnp.dot(og_ref[...], wout_ref[...],
                                      preferred_element_type=jnp.float32)


def _attention(q, kt, va, gs, x2, wout, batch, seq, tq):
    nq = seq // tq
    width = N_HEADS * HEAD_PAD
    qtile = lambda w: pl.BlockSpec((tq, w), lambda b, i: (b * nq + i, 0))
    return pl.pallas_call(
        functools.partial(_attn_kernel, tq=tq, nq=nq),
        grid=(batch, nq),
        in_specs=[qtile(width),
                  pl.BlockSpec((width, seq), lambda b, i: (b, 0)),
                  pl.BlockSpec((seq, width), lambda b, i: (b, 0)),
                  qtile(ATTN_WIDTH), qtile(D_MODEL),
                  pl.BlockSpec(wout.shape, lambda b, i: (0, 0))],
        out_specs=qtile(D_MODEL),
        out_shape=jax.ShapeDtypeStruct(x2.shape, jnp.float32),
        scratch_shapes=[pltpu.VMEM((tq, ATTN_WIDTH), jnp.bfloat16)],
        compiler_params=pltpu.CompilerParams(
            dimension_semantics=("arbitrary", "arbitrary"), vmem_limit_bytes=VMEM_LIMIT),
        name="mla_attn",
    )(q, kt, va, gs, x2, wout)


def _conv_kernel(x_ref, g_ref, win_ref, bin_ref, dw_ref, dwb_ref, lng_ref, lnb_ref, wout_ref,
                 bout_ref, fg_ref, o_ref, u_ref, c_ref, y_ref, winb_ref, woutb_ref, *, tm,
                 final_norm):
    nchunk = D_MODEL // LANES
    rows = 64

    @pl.when((pl.program_id(0) == 0) & (pl.program_id(1) == 0))
    def _():
        winb_ref[...] = win_ref[...].astype(jnp.bfloat16)
        woutb_ref[...] = wout_ref[...].astype(jnp.bfloat16)

    @pl.when(pl.program_id(1) == 0)
    def _():
        u_ref[:, 0:CONV_HALO, :] = jnp.zeros((nchunk, CONV_HALO, LANES), jnp.float32)

    x = x_ref[...]
    h = _rms(x, g_ref[...]).astype(jnp.bfloat16)

    def proj(lo, width):
        return (jnp.dot(h, winb_ref[:, lo:lo + width], preferred_element_type=jnp.float32)
                + bin_ref[:, lo:lo + width])

    group = 2 * LANES
    for gi in range(D_MODEL // group):
        u = proj(gi * group, group) * jax.nn.sigmoid(proj(D_MODEL + gi * group, group))
        for half in range(group // LANES):
            c = gi * (group // LANES) + half
            u_ref[c, CONV_HALO:CONV_HALO + tm, :] = u[:, half * LANES:(half + 1) * LANES]
            w = dw_ref[c]
            for r in range(tm // rows):
                acc = jnp.zeros((rows, LANES), jnp.float32)
                for k in range(CONV_K):
                    lo = CONV_HALO - (CONV_K - 1) + k + r * rows
                    acc = acc + u_ref[c, lo:lo + rows, :] * w[k:k + 1, :]
                c_ref[c, r * rows:(r + 1) * rows, :] = acc
            u_ref[c, 0:CONV_HALO, :] = u_ref[c, tm:tm + CONV_HALO, :]

    cv = jnp.concatenate([c_ref[c] for c in range(nchunk)], axis=-1) + dwb_ref[...]
    mu = jnp.mean(cv, axis=-1, keepdims=True)
    d = cv - mu
    var = jnp.mean(d * d, axis=-1, keepdims=True)
    yn = d * lax.rsqrt(var + LN_EPS) * lng_ref[...] + lnb_ref[...]
    gate = proj(2 * D_MODEL, D_MODEL)
    y = (yn * jax.nn.sigmoid(yn)) * (gate * jax.nn.sigmoid(gate))
    y_ref[...] = y.astype(jnp.bfloat16)
    out = (x + jnp.dot(y_ref[...], woutb_ref[...], preferred_element_type=jnp.float32)
           + bout_ref[...])
    if final_norm:
        out = _rms(out, fg_ref[...])
    o_ref[...] = out


def _conv_layer(x2, g, win_all, b_in, dw3, dw_b, ln_g, ln_b, wout_all, b_out, fg, layer, batch, seq,
                tm, final_norm):
    ns = seq // tm
    nchunk = D_MODEL // LANES
    tile = pl.BlockSpec((tm, D_MODEL), lambda b, i: (b * ns + i, 0))
    full = lambda a: pl.BlockSpec(a.shape, lambda b, i: (0,) * a.ndim)
    stacked = lambda a: pl.BlockSpec((None,) + a.shape[1:], lambda b, i: (layer, 0, 0),
                                     pipeline_mode=pl.Buffered(1))
    args = (x2, g, win_all, b_in, dw3, dw_b, ln_g, ln_b, wout_all, b_out, fg)
    return pl.pallas_call(
        functools.partial(_conv_kernel, tm=tm, final_norm=final_norm),
        grid=(batch, ns),
        in_specs=[tile, full(g), stacked(win_all), full(b_in), full(dw3), full(dw_b), full(ln_g),
                  full(ln_b), stacked(wout_all), full(b_out), full(fg)],
        out_specs=tile,
        out_shape=jax.ShapeDtypeStruct(x2.shape, jnp.float32),
        scratch_shapes=[pltpu.VMEM((nchunk, tm + CONV_HALO, LANES), jnp.float32),
                        pltpu.VMEM((nchunk, tm, LANES), jnp.float32),
                        pltpu.VMEM((tm, D_MODEL), jnp.bfloat16),
                        pltpu.VMEM(win_all.shape[1:], jnp.bfloat16),
                        pltpu.VMEM(wout_all.shape[1:], jnp.bfloat16)],
        compiler_params=pltpu.CompilerParams(
            dimension_semantics=("arbitrary", "arbitrary"), vmem_limit_bytes=VMEM_LIMIT),
        name="conv_layer",
    )(*args)


def _mla_weights(w_in, w_qb, w_kvb, w_out):
    bf = jnp.bfloat16
    k_pe = w_in[:, Q_LORA + KV_LORA:Q_LORA + KV_LORA + QK_ROPE]
    win = jnp.concatenate([w_in[:, :Q_LORA + KV_LORA], k_pe, k_pe,
                           w_in[:, Q_LORA + KV_LORA + QK_ROPE:]], axis=1).astype(bf)
    wq = w_qb.reshape(Q_LORA, N_HEADS, QK_NOPE + QK_ROPE)
    wqb = jnp.concatenate([wq[..., :QK_NOPE], wq[..., QK_NOPE:], wq[..., QK_NOPE:]],
                          axis=-1).reshape(Q_LORA, N_HEADS * HEAD_PAD).astype(bf)
    wkv = w_kvb.reshape(KV_LORA, N_HEADS, QK_NOPE + V_HEAD)
    wkt = wkv[..., :QK_NOPE].reshape(KV_LORA, -1).T.astype(bf)
    wv = wkv[..., QK_NOPE:].reshape(KV_LORA, -1).astype(bf)
    return win, wqb, wkt, wv, w_out.astype(bf)


def kernel(x, positions, final_norm_g, mla_norm_g, mla_w_in, mla_q_norm_g, mla_w_qb, mla_kv_norm_g, mla_w_kvb, mla_w_out, conv_norm_g, conv_w_in, conv_b_in, conv_dw_w, conv_dw_b, conv_ln_g, conv_ln_b, conv_w_out, conv_b_out):
    cfg = _tile_config()
    batch, seq, d = x.shape
    depth = mla_norm_g.shape[0] + conv_norm_g.shape[0]
    assert d == D_MODEL and seq % max(cfg.values()) == 0 and depth % 2 == 0
    nchunk = D_MODEL // LANES
    row = lambda a: a.reshape(1, -1).astype(jnp.float32)

    c_tab, s_tab = _rope_tables(positions, cfg["rope_tm"])
    x2 = x.reshape(batch * seq, D_MODEL)
    for i in range(depth):
        j = i // 2
        if i % 2 == 0:
            win, wqb, wkt, wv, wout = _mla_weights(mla_w_in[j], mla_w_qb[j], mla_w_kvb[j],
                                                   mla_w_out[j])
            q, kt, va, gs = _mla_proj(x2, row(mla_norm_g[j]), win, row(mla_q_norm_g[j]), wqb,
                                      row(mla_kv_norm_g[j]), wkt, wv, c_tab, s_tab, batch, seq,
                                      cfg["proj_tm"])
            x2 = _attention(q, kt, va, gs, x2, wout, batch, seq, cfg["attn_tq"])
        else:
            dw = jnp.pad(conv_dw_w[j], ((0, CONV_HALO - CONV_K), (0, 0)))
            dw3 = dw.reshape(CONV_HALO, nchunk, LANES).transpose(1, 0, 2)
            x2 = _conv_layer(x2, row(conv_norm_g[j]), conv_w_in, row(conv_b_in[j]), dw3,
                             row(conv_dw_b[j]), row(conv_ln_g[j]), row(conv_ln_b[j]), conv_w_out,
                             row(conv_b_out[j]), row(final_norm_g), j, batch, seq, cfg["conv_tm"],
                             final_norm=(i == depth - 1))
    return x2.reshape(batch, seq, D_MODEL)
```

```python
import functools
import math

import jax
import jax.numpy as jnp
from jax import lax
from jax.experimental import pallas as pl
from jax.experimental.pallas import tpu as pltpu

D_MODEL = 1024
N_HEADS = 8
QK_NOPE = 128
QK_ROPE = 64
V_HEAD = 128
Q_LORA = 384
KV_LORA = 256
ATTN_WIDTH = N_HEADS * V_HEAD
ROPE_THETA = 10000.0
CONV_K = 31
RMS_EPS = 1e-6
LN_EPS = 1e-5

LANES = 128
HEAD_PAD = 2 * LANES
HALF_ROPE = QK_ROPE // 2
CONV_HALO = 32
VMEM_LIMIT = 56 * 1024 * 1024


def _tile_config():
    return dict(rope_tm=512, proj_tm=512, attn_tq=256, conv_tm=1024)


def _rms(x, g):
    return x * lax.rsqrt(jnp.mean(x * x, axis=-1, keepdims=True) + RMS_EPS) * g


def _rope_kernel(pos_ref, freq_ref, cmask_ref, smask_ref, c_ref, s_ref):
    half = pos_ref.shape[0] // 2
    lane = lax.broadcasted_iota(jnp.int32, (half, LANES), 1)
    pos = jnp.where(lane < QK_ROPE,
                    jnp.broadcast_to(pos_ref[0:half, :], (half, LANES)),
                    jnp.broadcast_to(pos_ref[half:, :], (half, LANES)))
    ang = pos.astype(jnp.float32) * freq_ref[...]
    c = jnp.cos(ang)
    s = jnp.sin(ang)
    cmask = cmask_ref[...]
    smask = smask_ref[...]
    c_ref[0:half, :] = c * cmask
    c_ref[half:, :] = pltpu.roll(c, QK_ROPE, 1) * cmask
    s_ref[0:half, :] = s * smask
    s_ref[half:, :] = pltpu.roll(s, QK_ROPE, 1) * smask


def _rope_tables(positions, tm):
    t = positions.size
    inv_freq = ROPE_THETA ** (-jnp.arange(0, QK_ROPE, 2, dtype=jnp.float32) / QK_ROPE)
    zeros = jnp.zeros((2 * HALF_ROPE,), jnp.float32)
    ones = jnp.ones((HALF_ROPE,), jnp.float32)
    freq = jnp.concatenate([inv_freq] * 4)[None, :]
    cmask = jnp.concatenate([ones, ones, zeros])[None, :]
    smask = jnp.concatenate([-ones, ones, zeros])[None, :]
    row = pl.BlockSpec((1, LANES), lambda i: (0, 0))
    tile = pl.BlockSpec((tm, LANES), lambda i: (i, 0))
    return pl.pallas_call(
        _rope_kernel,
        grid=(t // tm,),
        in_specs=[pl.BlockSpec((tm, 1), lambda i: (i, 0)), row, row, row],
        out_specs=[tile, tile],
        out_shape=[jax.ShapeDtypeStruct((t, LANES), jnp.float32)] * 2,
        name="rope_tables",
    )(positions.reshape(t, 1), freq, cmask, smask)


def _rope(x, c, s):
    return x * c + pltpu.roll(x, HALF_ROPE, 1) * s


def _mla_proj_kernel(x_ref, g_ref, win_ref, gq_ref, wqb_ref, gkv_ref, wkt_ref, wv_ref, c_ref, s_ref,
                     q_ref, kt_ref, va_ref, gs_ref):
    tm = x_ref.shape[0]
    h = _rms(x_ref[...], g_ref[...]).astype(jnp.bfloat16)
    z = jnp.dot(h, win_ref[...], preferred_element_type=jnp.float32)
    qn = _rms(z[:, :Q_LORA], gq_ref[...]).astype(jnp.bfloat16)
    kvn = _rms(z[:, Q_LORA:Q_LORA + KV_LORA], gkv_ref[...])
    q = jnp.dot(qn, wqb_ref[...], preferred_element_type=jnp.float32)
    v = jnp.dot(kvn.astype(jnp.bfloat16), wv_ref[...], preferred_element_type=jnp.float32)
    k_t = jnp.dot(wkt_ref[...], kvn.T.astype(jnp.bfloat16),
                  preferred_element_type=jnp.float32)
    c = c_ref[...]
    s = s_ref[...]
    kpe_t = _rope(z[:, Q_LORA + KV_LORA:Q_LORA + KV_LORA + LANES], c, s).T.astype(jnp.bfloat16)
    gate = z[:, Q_LORA + KV_LORA + LANES:]
    gs_ref[...] = (gate * jax.nn.sigmoid(gate)).astype(jnp.bfloat16)
    scale = math.log2(math.e) / math.sqrt(QK_NOPE + QK_ROPE)
    ones = jnp.ones((tm, LANES), jnp.bfloat16)
    for hd in range(N_HEADS):
        lo = hd * HEAD_PAD
        q_ref[:, lo:lo + LANES] = (q[:, lo:lo + LANES] * scale).astype(jnp.bfloat16)
        q_ref[:, lo + LANES:lo + HEAD_PAD] = (
            _rope(q[:, lo + LANES:lo + HEAD_PAD], c, s) * scale).astype(jnp.bfloat16)
        kt_ref[lo:lo + LANES, :] = k_t[hd * QK_NOPE:(hd + 1) * QK_NOPE, :].astype(jnp.bfloat16)
        kt_ref[lo + LANES:lo + HEAD_PAD, :] = kpe_t
        va_ref[:, lo:lo + LANES] = v[:, hd * V_HEAD:(hd + 1) * V_HEAD].astype(jnp.bfloat16)
        va_ref[:, lo + LANES:lo + HEAD_PAD] = ones


def _mla_proj(x2, g, win, gq, wqb, gkv, wkt, wv, c_tab, s_tab, batch, seq, tm):
    t = x2.shape[0]
    ns = seq // tm
    width = N_HEADS * HEAD_PAD
    full = lambda a: pl.BlockSpec(a.shape, lambda b, i: (0, 0))
    tile = lambda w: pl.BlockSpec((tm, w), lambda b, i: (b * ns + i, 0))
    return pl.pallas_call(
        _mla_proj_kernel,
        grid=(batch, ns),
        in_specs=[tile(D_MODEL), full(g), full(win), full(gq), full(wqb), full(gkv), full(wkt),
                  full(wv), tile(LANES), tile(LANES)],
        out_specs=[tile(width), pl.BlockSpec((width, tm), lambda b, i: (b, i)), tile(width),
                   tile(ATTN_WIDTH)],
        out_shape=[jax.ShapeDtypeStruct((t, width), jnp.bfloat16),
                   jax.ShapeDtypeStruct((batch * width, seq), jnp.bfloat16),
                   jax.ShapeDtypeStruct((t, width), jnp.bfloat16),
                   jax.ShapeDtypeStruct((t, ATTN_WIDTH), jnp.bfloat16)],
        compiler_params=pltpu.CompilerParams(
            dimension_semantics=("arbitrary", "arbitrary"), vmem_limit_bytes=VMEM_LIMIT),
        name="mla_proj",
    )(x2, g, win, gq, wqb, gkv, wkt, wv, c_tab, s_tab)


def _attn_head(qh, kt_ref, va_ref, hd, c, tq, causal):
    rows = slice(hd * HEAD_PAD, (hd + 1) * HEAD_PAD)
    past = c * tq
    s_d = jnp.dot(qh, kt_ref[rows, past:past + tq], preferred_element_type=jnp.float32)
    s_d = jnp.where(causal, s_d, -jnp.inf)
    m = jnp.max(s_d, axis=-1, keepdims=True)
    if past:
        s_p = jnp.dot(qh, kt_ref[rows, 0:past], preferred_element_type=jnp.float32)
        m = jnp.maximum(m, jnp.max(s_p, axis=-1, keepdims=True))
        p_p = jnp.exp2(s_p - m).astype(jnp.bfloat16)
    p_d = jnp.exp2(s_d - m).astype(jnp.bfloat16)
    pv = jnp.dot(p_d, va_ref[past:past + tq, rows], preferred_element_type=jnp.float32)
    if past:
        pv = pv + jnp.dot(p_p, va_ref[0:past, rows], preferred_element_type=jnp.float32)
    return pv[:, :V_HEAD] / pv[:, V_HEAD:]


def _attn_kernel(q_ref, kt_ref, va_ref, gs_ref, x_ref, wout_ref, o_ref, og_ref, *, tq, nq):
    qi = pl.program_id(1)
    row = lax.broadcasted_iota(jnp.int32, (tq, tq), 0)
    col = lax.broadcasted_iota(jnp.int32, (tq, tq), 1)
    causal = col <= row

    for c in range(nq):
        @pl.when(qi == c)
        def _(c=c):
            for hd in range(N_HEADS):
                qh = q_ref[:, hd * HEAD_PAD:(hd + 1) * HEAD_PAD]
                o = _attn_head(qh, kt_ref, va_ref, hd, c, tq, causal)
                gsh = gs_ref[:, hd * V_HEAD:(hd + 1) * V_HEAD].astype(jnp.float32)
                og_ref[:, hd * V_HEAD:(hd + 1) * V_HEAD] = (o * gsh).astype(jnp.bfloat16)

    o_ref[...] = x_ref[...] + jnp.dot(og_ref[...], wout_ref[...],
                                      preferred_element_type=jnp.float32)


def _attention(q, kt, va, gs, x2, wout, batch, seq, tq):
    nq = seq // tq
    width = N_HEADS * HEAD_PAD
    qtile = lambda w: pl.BlockSpec((tq, w), lambda b, i: (b * nq + i, 0))
    return pl.pallas_call(
        functools.partial(_attn_kernel, tq=tq, nq=nq),
        grid=(batch, nq),
        in_specs=[qtile(width),
                  pl.BlockSpec((width, seq), lambda b, i: (b, 0)),
                  pl.BlockSpec((seq, width), lambda b, i: (b, 0)),
                  qtile(ATTN_WIDTH), qtile(D_MODEL),
                  pl.BlockSpec(wout.shape, lambda b, i: (0, 0))],
        out_specs=qtile(D_MODEL),
        out_shape=jax.ShapeDtypeStruct(x2.shape, jnp.float32),
        scratch_shapes=[pltpu.VMEM((tq, ATTN_WIDTH), jnp.bfloat16)],
        compiler_params=pltpu.CompilerParams(
            dimension_semantics=("arbitrary", "arbitrary"), vmem_limit_bytes=VMEM_LIMIT),
        name="mla_attn",
    )(q, kt, va, gs, x2, wout)


def _conv_kernel(x_ref, g_ref, win_ref, bin_ref, dw_ref, dwb_ref, lng_ref, lnb_ref, wout_ref,
                 bout_ref, fg_ref, o_ref, u_ref, c_ref, y_ref, winb_ref, woutb_ref, *, tm,
                 final_norm):
    nchunk = D_MODEL // LANES
    rows = 64

    @pl.when((pl.program_id(0) == 0) & (pl.program_id(1) == 0))
    def _():
        winb_ref[...] = win_ref[...].astype(jnp.bfloat16)
        woutb_ref[...] = wout_ref[...].astype(jnp.bfloat16)

    @pl.when(pl.program_id(1) == 0)
    def _():
        u_ref[:, 0:CONV_HALO, :] = jnp.zeros((nchunk, CONV_HALO, LANES), jnp.float32)

    x = x_ref[...]
    h = _rms(x, g_ref[...]).astype(jnp.bfloat16)

    def proj(lo, width):
        return (jnp.dot(h, winb_ref[:, lo:lo + width], preferred_element_type=jnp.float32)
                + bin_ref[:, lo:lo + width])

    group = 2 * LANES
    for gi in range(D_MODEL // group):
        u = proj(gi * group, group) * jax.nn.sigmoid(proj(D_MODEL + gi * group, group))
        for half in range(group // LANES):
            c = gi * (group // LANES) + half
            u_ref[c, CONV_HALO:CONV_HALO + tm, :] = u[:, half * LANES:(half + 1) * LANES]
            w = dw_ref[c]
            for r in range(tm // rows):
                acc = jnp.zeros((rows, LANES), jnp.float32)
                for k in range(CONV_K):
                    lo = CONV_HALO - (CONV_K - 1) + k + r * rows
                    acc = acc + u_ref[c, lo:lo + rows, :] * w[k:k + 1, :]
                c_ref[c, r * rows:(r + 1) * rows, :] = acc
            u_ref[c, 0:CONV_HALO, :] = u_ref[c, tm:tm + CONV_HALO, :]

    cv = jnp.concatenate([c_ref[c] for c in range(nchunk)], axis=-1) + dwb_ref[...]
    mu = jnp.mean(cv, axis=-1, keepdims=True)
    d = cv - mu
    var = jnp.mean(d * d, axis=-1, keepdims=True)
    yn = d * lax.rsqrt(var + LN_EPS) * lng_ref[...] + lnb_ref[...]
    gate = proj(2 * D_MODEL, D_MODEL)
    y = (yn * jax.nn.sigmoid(yn)) * (gate * jax.nn.sigmoid(gate))
    y_ref[...] = y.astype(jnp.bfloat16)
    out = (x + jnp.dot(y_ref[...], woutb_ref[...], preferred_element_type=jnp.float32)
           + bout_ref[...])
    if final_norm:
        out = _rms(out, fg_ref[...])
    o_ref[...] = out


def _conv_layer(x2, g, win_all, b_in, dw3, dw_b, ln_g, ln_b, wout_all, b_out, fg, layer, batch, seq,
                tm, final_norm):
    ns = seq // tm
    nchunk = D_MODEL // LANES
    tile = pl.BlockSpec((tm, D_MODEL), lambda b, i: (b * ns + i, 0))
    full = lambda a: pl.BlockSpec(a.shape, lambda b, i: (0,) * a.ndim)
    stacked = lambda a: pl.BlockSpec((None,) + a.shape[1:], lambda b, i: (layer, 0, 0),
                                     pipeline_mode=pl.Buffered(1))
    args = (x2, g, win_all, b_in, dw3, dw_b, ln_g, ln_b, wout_all, b_out, fg)
    return pl.pallas_call(
        functools.partial(_conv_kernel, tm=tm, final_norm=final_norm),
        grid=(batch, ns),
        in_specs=[tile, full(g), stacked(win_all), full(b_in), full(dw3), full(dw_b), full(ln_g),
                  full(ln_b), stacked(wout_all), full(b_out), full(fg)],
        out_specs=tile,
        out_shape=jax.ShapeDtypeStruct(x2.shape, jnp.float32),
        scratch_shapes=[pltpu.VMEM((nchunk, tm + CONV_HALO, LANES), jnp.float32),
                        pltpu.VMEM((nchunk, tm, LANES), jnp.float32),
                        pltpu.VMEM((tm, D_MODEL), jnp.bfloat16),
                        pltpu.VMEM(win_all.shape[1:], jnp.bfloat16),
                        pltpu.VMEM(wout_all.shape[1:], jnp.bfloat16)],
        compiler_params=pltpu.CompilerParams(
            dimension_semantics=("arbitrary", "arbitrary"), vmem_limit_bytes=VMEM_LIMIT),
        name="conv_layer",
    )(*args)


def _mla_weights(w_in, w_qb, w_kvb, w_out):
    bf = jnp.bfloat16
    k_pe = w_in[:, Q_LORA + KV_LORA:Q_LORA + KV_LORA + QK_ROPE]
    win = jnp.concatenate([w_in[:, :Q_LORA + KV_LORA], k_pe, k_pe,
                           w_in[:, Q_LORA + KV_LORA + QK_ROPE:]], axis=1).astype(bf)
    wq = w_qb.reshape(Q_LORA, N_HEADS, QK_NOPE + QK_ROPE)
    wqb = jnp.concatenate([wq[..., :QK_NOPE], wq[..., QK_NOPE:], wq[..., QK_NOPE:]],
                          axis=-1).reshape(Q_LORA, N_HEADS * HEAD_PAD).astype(bf)
    wkv = w_kvb.reshape(KV_LORA, N_HEADS, QK_NOPE + V_HEAD)
    wkt = wkv[..., :QK_NOPE].reshape(KV_LORA, -1).T.astype(bf)
    wv = wkv[..., QK_NOPE:].reshape(KV_LORA, -1).astype(bf)
    return win, wqb, wkt, wv, w_out.astype(bf)


def kernel(x, positions, final_norm_g, mla_norm_g, mla_w_in, mla_q_norm_g, mla_w_qb, mla_kv_norm_g, mla_w_kvb, mla_w_out, conv_norm_g, conv_w_in, conv_b_in, conv_dw_w, conv_dw_b, conv_ln_g, conv_ln_b, conv_w_out, conv_b_out):
    cfg = _tile_config()
    batch, seq, d = x.shape
    depth = mla_norm_g.shape[0] + conv_norm_g.shape[0]
    assert d == D_MODEL and seq % max(cfg.values()) == 0 and depth % 2 == 0
    nchunk = D_MODEL // LANES
    row = lambda a: a.reshape(1, -1).astype(jnp.float32)

    c_tab, s_tab = _rope_tables(positions, cfg["rope_tm"])
    x2 = x.reshape(batch * seq, D_MODEL)
    for i in range(depth):
        j = i // 2
        if i % 2 == 0:
            win, wqb, wkt, wv, wout = _mla_weights(mla_w_in[j], mla_w_qb[j], mla_w_kvb[j],
                                                   mla_w_out[j])
            q, kt, va, gs = _mla_proj(x2, row(mla_norm_g[j]), win, row(mla_q_norm_g[j]), wqb,
                                      row(mla_kv_norm_g[j]), wkt, wv, c_tab, s_tab, batch, seq,
                                      cfg["proj_tm"])
            x2 = _attention(q, kt, va, gs, x2, wout, batch, seq, cfg["attn_tq"])
        else:
            dw = jnp.pad(conv_dw_w[j], ((0, CONV_HALO - CONV_K), (0, 0)))
            dw3 = dw.reshape(CONV_HALO, nchunk, LANES).transpose(1, 0, 2)
            x2 = _conv_layer(x2, row(conv_norm_g[j]), conv_w_in, row(conv_b_in[j]), dw3,
                             row(conv_dw_b[j]), row(conv_ln_g[j]), row(conv_ln_b[j]), conv_w_out,
                             row(conv_b_out[j]), row(final_norm_g), j, batch, seq, cfg["conv_tm"],
                             final_norm=(i == depth - 1))
    return x2.reshape(batch, seq, D_MODEL)
```
